```python
import jax, jax.numpy as jnp
from jax import lax
import numpy as np

D_MODEL = 2048
BATCH = 2
SEQ = 8192
DEPTH = 1
DEC_BATCH = 1
DEC_SEQ = 8192
PAST_LEN = 128

N_HEADS = 16
Q_LORA = 512
KV_LORA = 512
QK_NOPE = 128
QK_ROPE = 64
QK_HEAD = QK_NOPE + QK_ROPE
V_HEAD = 128
ROPE_THETA = 10000.0
Q_BLOCK = 128
GMLP_GROUPS = 16
GMLP_GROUP_DIM = 128
GMLP_DIM = GMLP_GROUPS * GMLP_GROUP_DIM
CHUNK = 128
D_FF = 5632
N_BRANCH = 2
EPS = 1e-6
IN_SPLITS = (Q_LORA, KV_LORA, QK_ROPE, 2 * GMLP_DIM, N_BRANCH * D_MODEL)
D_IN = Q_LORA + KV_LORA + QK_ROPE + 2 * GMLP_DIM + N_BRANCH * D_MODEL

kernel_name = "mla_gmlp_macaron_encoder"


def rmsnorm(x, g):
    xf = x.astype(jnp.float32)
    y = xf * lax.rsqrt(jnp.mean(xf * xf, axis=-1, keepdims=True) + EPS)
    return (y * g.astype(jnp.float32)).astype(x.dtype)


def swiglu(x, w_gate, w_up, w_down):
    return (jax.nn.silu(x @ w_gate) * (x @ w_up)) @ w_down


def rope_tables(seq_len):
    inv = 1.0 / (ROPE_THETA ** (jnp.arange(0, QK_ROPE, 2, dtype=jnp.float32) / QK_ROPE))
    ang = jnp.arange(seq_len, dtype=jnp.float32)[:, None] * inv[None, :]
    return jnp.cos(ang)[:, None, :], jnp.sin(ang)[:, None, :]


def apply_rope(x, cos, sin):
    x1, x2 = jnp.split(x.astype(jnp.float32), 2, axis=-1)
    out = jnp.concatenate([x1 * cos - x2 * sin, x2 * cos + x1 * sin], axis=-1)
    return out.astype(x.dtype)


def mla_attention(q_c, kv_c, k_r, g_q, g_kv, w_uq, w_uk, w_uv, cos, sin):
    b, s, _ = q_c.shape
    q = (rmsnorm(q_c, g_q) @ w_uq).reshape(b, s, N_HEADS, QK_HEAD)
    q_nope = q[..., :QK_NOPE]
    q_rope = apply_rope(q[..., QK_NOPE:], cos, sin)
    c_kv = rmsnorm(kv_c, g_kv)
    k_nope = (c_kv @ w_uk).reshape(b, s, N_HEADS, QK_NOPE)
    v = (c_kv @ w_uv).reshape(b, s, N_HEADS, V_HEAD)
    k_rope = apply_rope(k_r[:, :, None, :], cos, sin)[:, :, 0]
    scale = QK_HEAD ** -0.5
    nb = s // Q_BLOCK
    qn_blk = q_nope.reshape(b, nb, Q_BLOCK, N_HEADS, QK_NOPE).transpose(1, 0, 2, 3, 4)
    qr_blk = q_rope.reshape(b, nb, Q_BLOCK, N_HEADS, QK_ROPE).transpose(1, 0, 2, 3, 4)

    def block(args):
        qn, qr = args
        logits = (jnp.einsum('bqhd,bkhd->bhqk', qn, k_nope)
                  + jnp.einsum('bqhr,bkr->bhqk', qr, k_rope)).astype(jnp.float32) * scale
        p = jax.nn.softmax(logits, axis=-1).astype(v.dtype)
        return jnp.einsum('bhqk,bkhd->bqhd', p, v)

    o = lax.map(block, (qn_blk, qr_blk))
    return o.transpose(1, 0, 2, 3, 4).reshape(b, s, N_HEADS * V_HEAD)


def spatial_gating(uv, g_v, w_s, b_s):
    b, s, _ = uv.shape
    z = jax.nn.gelu(uv)
    u, v = jnp.split(z, 2, axis=-1)
    v = rmsnorm(v, g_v).reshape(b, s // CHUNK, CHUNK, GMLP_GROUPS, GMLP_GROUP_DIM)
    v = jnp.einsum('gpq,bnqgc->bnpgc', w_s, v) + b_s.T[None, None, :, :, None]
    return u * v.reshape(b, s, GMLP_DIM)


def parallel_mixer(h, w_in, b_gate, g_q, w_uq, g_kv, w_uk, w_uv, w_o_attn,
                   g_v, w_s, b_s, w_o_gmlp, w_out, cos, sin):
    b, s, _ = h.shape
    proj = h @ w_in
    o1 = IN_SPLITS[0]
    o2 = o1 + IN_SPLITS[1]
    o3 = o2 + IN_SPLITS[2]
    o4 = o3 + IN_SPLITS[3]
    q_c, kv_c, k_r, uv, gate_pre = (proj[..., :o1], proj[..., o1:o2], proj[..., o2:o3],
                                     proj[..., o3:o4], proj[..., o4:])
    a = mla_attention(q_c, kv_c, k_r, g_q, g_kv, w_uq, w_uk, w_uv, cos, sin) @ w_o_attn
    c = spatial_gating(uv, g_v, w_s, b_s) @ w_o_gmlp
    gates = jax.nn.sigmoid(gate_pre + b_gate).reshape(b, s, N_BRANCH, D_MODEL)
    merged = gates[:, :, 0] * a + gates[:, :, 1] * c
    return merged @ w_out


def encoder(x, layer_weights, final_norm):
    cos, sin = rope_tables(x.shape[1])
    for l in range(DEPTH):
        (f1n, f1g, f1u, f1d, mn, w_in, b_gate, g_q, w_uq, g_kv, w_uk, w_uv, w_oa,
         g_v, w_s, b_s, w_og, w_out, f2n, f2g, f2u, f2d) = [w[l] for w in layer_weights]
        x = x + 0.5 * swiglu(rmsnorm(x, f1n), f1g, f1u, f1d)
        x = x + parallel_mixer(rmsnorm(x, mn), w_in, b_gate, g_q, w_uq, g_kv, w_uk, w_uv, w_oa,
                               g_v, w_s, b_s, w_og, w_out, cos, sin)
        x = x + 0.5 * swiglu(rmsnorm(x, f2n), f2g, f2u, f2d)
    return rmsnorm(x, final_norm)


def setup_inputs(seed: int = 0) -> dict:
    key = jax.random.key(seed)
    ks = jax.random.split(key, 32)

    def dense(k, shape, fan_in):
        return jax.random.normal(k, shape, jnp.float32) * (fan_in ** -0.5)

    def gain(k, shape):
        return 1.0 + 0.02 * jax.random.normal(k, shape, jnp.float32)

    L, D, F = DEPTH, D_MODEL, D_FF
    return {
        "x_prompt": jax.random.normal(ks[0], (BATCH, SEQ, D), jnp.float32),
        "x_sample": jax.random.normal(ks[1], (DEC_BATCH, DEC_SEQ, D), jnp.float32),
        "ffn1_norm": gain(ks[2], (L, D)),
        "ffn1_w_gate": dense(ks[3], (L, D, F), D),
        "ffn1_w_up": dense(ks[4], (L, D, F), D),
        "ffn1_w_down": dense(ks[5], (L, F, D), F),
        "mix_norm": gain(ks[6], (L, D)),
        "w_in": dense(ks[7], (L, D, D_IN), D),
        "b_gate": 0.02 * jax.random.normal(ks[8], (L, N_BRANCH * D), jnp.float32),
        "q_norm": gain(ks[9], (L, Q_LORA)),
        "w_uq": dense(ks[10], (L, Q_LORA, N_HEADS * QK_HEAD), Q_LORA),
        "kv_norm": gain(ks[11], (L, KV_LORA)),
        "w_uk": dense(ks[12], (L, KV_LORA, N_HEADS * QK_NOPE), KV_LORA),
        "w_uv": dense(ks[13], (L, KV_LORA, N_HEADS * V_HEAD), KV_LORA),
        "w_o_attn": dense(ks[14], (L, N_HEADS * V_HEAD, D), N_HEADS * V_HEAD),
        "gmlp_norm": gain(ks[15], (L, GMLP_DIM)),
        "w_s": dense(ks[16], (L, GMLP_GROUPS, CHUNK, CHUNK), CHUNK),
        "b_s": 1.0 + 0.02 * jax.random.normal(ks[17], (L, GMLP_GROUPS, CHUNK), jnp.float32),
        "w_o_gmlp": dense(ks[18], (L, GMLP_DIM, D), GMLP_DIM),
        "w_out": dense(ks[19], (L, D, D), D),
        "ffn2_norm": gain(ks[20], (L, D)),
        "ffn2_w_gate": dense(ks[21], (L, D, F), D),
        "ffn2_w_up": dense(ks[22], (L, D, F), D),
        "ffn2_w_down": dense(ks[23], (L, F, D), F),
        "final_norm": gain(ks[24], (D,)),
    }


def reference(x_prompt, x_sample, ffn1_norm, ffn1_w_gate, ffn1_w_up, ffn1_w_down,
              mix_norm, w_in, b_gate, q_norm, w_uq, kv_norm, w_uk, w_uv, w_o_attn,
              gmlp_norm, w_s, b_s, w_o_gmlp, w_out,
              ffn2_norm, ffn2_w_gate, ffn2_w_up, ffn2_w_down, final_norm):
    layer_weights = (ffn1_norm, ffn1_w_gate, ffn1_w_up, ffn1_w_down,
                     mix_norm, w_in, b_gate, q_norm, w_uq, kv_norm, w_uk, w_uv, w_o_attn,
                     gmlp_norm, w_s, b_s, w_o_gmlp, w_out,
                     ffn2_norm, ffn2_w_gate, ffn2_w_up, ffn2_w_down)
    y_prompt = encoder(x_prompt, layer_weights, final_norm)
    y_sample = encoder(x_sample, layer_weights, final_norm)
    return (y_prompt, y_sample)
```

```python
import functools
import math

import jax
import jax.numpy as jnp
from jax import lax
from jax.experimental import pallas as pl
from jax.experimental.pallas import tpu as pltpu

D_MODEL = 2048
N_HEADS = 16
Q_LORA = 512
KV_LORA = 512
QK_NOPE = 128
QK_ROPE = 64
QK_HEAD = QK_NOPE + QK_ROPE
V_HEAD = 128
ROPE_THETA = 10000.0
GMLP_GROUPS = 16
GMLP_GROUP_DIM = 128
GMLP_DIM = GMLP_GROUPS * GMLP_GROUP_DIM
CHUNK = 128
D_FF = 5632
EPS = 1e-6

LANES = 128
QK_PAD = 2 * LANES
VMEM_LIMIT = 56 * 1024 * 1024

F32 = jnp.float32
BF16 = jnp.bfloat16


def _rmsnorm(x, g):
    y = x * lax.rsqrt(jnp.mean(x * x, axis=-1, keepdims=True) + EPS)
    return y * g


def _dot(a, b):
    return jnp.dot(a, b, preferred_element_type=F32)


def _dot_nt(a, b):
    return lax.dot_general(a, b, (((1,), (1,)), ((), ())), preferred_element_type=F32)


def _params(sem):
    return pltpu.CompilerParams(dimension_semantics=sem, vmem_limit_bytes=VMEM_LIMIT)


def _resident(shape):
    zeros = (0,) * len(shape)
    return pl.BlockSpec(shape, lambda *_: zeros, pipeline_mode=pl.Buffered(1))


def _ffn_kernel(x_ref, g_ref, wg_ref, wu_ref, wd_ref, fg_ref, o_ref, h_ref, *, nj, final_norm):
    j = pl.program_id(1)

    @pl.when(j == 0)
    def _():
        h_ref[...] = _rmsnorm(x_ref[...], g_ref[...]).astype(BF16)
        o_ref[...] = jnp.zeros_like(o_ref)

    h = h_ref[...]
    a = _dot(h, wg_ref[...])
    b = _dot(h, wu_ref[...])
    act = (a * jax.nn.sigmoid(a) * b).astype(BF16)
    o_ref[...] += _dot(act, wd_ref[...])

    @pl.when(j == nj - 1)
    def _():
        y = x_ref[...] + 0.5 * o_ref[...]
        if final_norm:
            y = _rmsnorm(y, fg_ref[...])
        o_ref[...] = y


def _ffn(x, g, wg, wu, wd, fg, *, final_norm, tm=512, tf=512):
    m, d = x.shape
    f = wg.shape[1]
    tm = min(tm, m)
    nj = f // tf
    return pl.pallas_call(
        functools.partial(_ffn_kernel, nj=nj, final_norm=final_norm),
        grid=(m // tm, nj),
        in_specs=[
            pl.BlockSpec((tm, d), lambda i, j: (i, 0)),
            pl.BlockSpec((1, d), lambda i, j: (0, 0)),
            pl.BlockSpec((d, tf), lambda i, j: (0, j)),
            pl.BlockSpec((d, tf), lambda i, j: (0, j)),
            pl.BlockSpec((tf, d), lambda i, j: (j, 0)),
            pl.BlockSpec((1, d), lambda i, j: (0, 0)),
        ],
        out_specs=pl.BlockSpec((tm, d), lambda i, j: (i, 0)),
        out_shape=jax.ShapeDtypeStruct((m, d), F32),
        scratch_shapes=[pltpu.VMEM((tm, d), BF16)],
        compiler_params=_params(("parallel", "arbitrary")),
        name="ffn",
    )(x, g, wg, wu, wd, fg)


def _rope_pair(t):
    return t + pltpu.roll(t, LANES // 2, axis=1)


def _proj_kernel(x_ref, mn_ref, wqkr_ref, gq_ref, wuq_ref, gkv_ref, wuk_ref, wuvt_ref, cs_ref,
                 q_ref, k_ref, vt_ref, *, q_scale):
    h = _rmsnorm(x_ref[0], mn_ref[...]).astype(BF16)
    p = _dot(h, wqkr_ref[...])
    cs = cs_ref[...]

    qn = _rmsnorm(p[:, :Q_LORA], gq_ref[...]).astype(BF16)
    q = _dot(qn, wuq_ref[...]) * q_scale
    for hd in range(N_HEADS):
        base = hd * QK_PAD
        q_ref[0, hd, :, :LANES] = q[:, base:base + LANES].astype(BF16)
        q_ref[0, hd, :, LANES:] = _rope_pair(q[:, base + LANES:base + QK_PAD] * cs).astype(BF16)

    ckv = _rmsnorm(p[:, Q_LORA:Q_LORA + KV_LORA], gkv_ref[...]).astype(BF16)
    kn = _dot(ckv, wuk_ref[...])
    kr = _rope_pair(p[:, Q_LORA + KV_LORA:] * cs)
    lane = lax.broadcasted_iota(jnp.int32, kr.shape, 1)
    kr = jnp.where(lane < QK_ROPE, kr, 0.0).astype(BF16)
    for hd in range(N_HEADS):
        k_ref[0, hd, :, :LANES] = kn[:, hd * LANES:(hd + 1) * LANES].astype(BF16)
        k_ref[0, hd, :, LANES:] = kr

    vt = _dot_nt(wuvt_ref[...], ckv)
    for hd in range(N_HEADS):
        vt_ref[0, hd] = vt[hd * V_HEAD:(hd + 1) * V_HEAD, :].astype(BF16)


def _proj(x, mn, wqkr, gq, wuq, gkv, wuk, wuvt, cs, *, tm=256):
    b, s, d = x.shape
    tm = min(tm, s)
    q_scale = (QK_HEAD ** -0.5) * math.log2(math.e)
    return pl.pallas_call(
        functools.partial(_proj_kernel, q_scale=q_scale),
        grid=(b, s // tm),
        in_specs=[
            pl.BlockSpec((1, tm, d), lambda bi, i: (bi, i, 0)),
            _resident(mn.shape),
            _resident(wqkr.shape),
            _resident(gq.shape),
            _resident(wuq.shape),
            _resident(gkv.shape),
            _resident(wuk.shape),
            _resident(wuvt.shape),
            pl.BlockSpec((tm, LANES), lambda bi, i: (i, 0)),
        ],
        out_specs=[
            pl.BlockSpec((1, N_HEADS, tm, QK_PAD), lambda bi, i: (bi, 0, i, 0)),
            pl.BlockSpec((1, N_HEADS, tm, QK_PAD), lambda bi, i: (bi, 0, i, 0)),
            pl.BlockSpec((1, N_HEADS, V_HEAD, tm), lambda bi, i: (bi, 0, 0, i)),
        ],
        out_shape=[
            jax.ShapeDtypeStruct((b, N_HEADS, s, QK_PAD), BF16),
            jax.ShapeDtypeStruct((b, N_HEADS, s, QK_PAD), BF16),
            jax.ShapeDtypeStruct((b, N_HEADS, V_HEAD, s), BF16),
        ],
        compiler_params=_params(("parallel", "parallel")),
        name="proj",
    )(x, mn, wqkr, gq, wuq, gkv, wuk, wuvt, cs)


def _attn_kernel(q_ref, k_ref, vt_ref, o_ref, *, tk, nk):
    q = q_ref[0, 0]
    tq = q.shape[0]

    def chunk(c, carry):
        m, l, acc = carry
        off = pl.multiple_of(c * tk, tk)
        st = _dot_nt(k_ref[0, 0, pl.ds(off, tk), :], q)
        m_new = jnp.maximum(m, jnp.max(st, axis=0, keepdims=True))
        alpha = jnp.exp2(m - m_new)
        p = jnp.exp2(st - m_new)
        l = alpha * l + jnp.sum(p, axis=0, keepdims=True)
        acc = alpha * acc + _dot(vt_ref[0, 0, :, pl.ds(off, tk)], p.astype(BF16))
        return m_new, l, acc

    init = (jnp.full((1, tq), -1e30, F32), jnp.zeros((1, tq), F32), jnp.zeros((V_HEAD, tq), F32))
    _, l, acc = lax.fori_loop(0, nk, chunk, init)
    o_ref[0] = (acc / l).T.astype(BF16)


def _attn(q, k, vt, *, tq=512, tk=512):
    b, nh, s, _ = q.shape
    tq = min(tq, s)
    tk = min(tk, s)
    return pl.pallas_call(
        functools.partial(_attn_kernel, tk=tk, nk=s // tk),
        grid=(b, nh, s // tq),
        in_specs=[
            pl.BlockSpec((1, 1, tq, QK_PAD), lambda bi, hi, qi: (bi, hi, qi, 0)),
            pl.BlockSpec((1, 1, s, QK_PAD), lambda bi, hi, qi: (bi, hi, 0, 0)),
            pl.BlockSpec((1, 1, V_HEAD, s), lambda bi, hi, qi: (bi, hi, 0, 0)),
        ],
        out_specs=pl.BlockSpec((1, tq, V_HEAD), lambda bi, hi, qi: (bi, qi, hi)),
        out_shape=jax.ShapeDtypeStruct((b, s, nh * V_HEAD), BF16),
        compiler_params=_params(("parallel", "parallel", "arbitrary")),
        name="attn",
    )(q, k, vt)


def _sgu_kernel(x_ref, mn_ref, w_ref, gv_ref, ws_ref, bs_ref, o_ref, h_ref, v_ref, ssq_ref, *, nj):
    j = pl.program_id(2)

    @pl.when(j == 0)
    def _():
        h_ref[...] = _rmsnorm(x_ref[0], mn_ref[...]).astype(BF16)
        ssq_ref[...] = jnp.zeros_like(ssq_ref)

    z = jax.nn.gelu(_dot(h_ref[...], w_ref[...]), approximate=True)

    @pl.when(j < nj)
    def _():
        v_ref[j] = z
        ssq_ref[...] += jnp.sum(z * z, axis=-1, keepdims=True)

    @pl.when(j >= nj)
    def _():
        r = lax.rsqrt(ssq_ref[...] * (1.0 / GMLP_DIM) + EPS)
        vn = (v_ref[j - nj] * r * gv_ref[...]).astype(BF16)
        tm, tn = vn.shape
        for c in range(tm // CHUNK):
            rows = slice(c * CHUNK, (c + 1) * CHUNK)
            for g in range(tn // GMLP_GROUP_DIM):
                cols = slice(g * GMLP_GROUP_DIM, (g + 1) * GMLP_GROUP_DIM)
                mixed = _dot(ws_ref[g], vn[rows, cols]) + bs_ref[g]
                o_ref[0, rows, cols] = (z[rows, cols] * mixed).astype(BF16)


def _sgu(x, mn, w_vu, gv, ws, bs_b, *, tm=512, tn=512):
    b, s, d = x.shape
    tm = min(tm, s)
    nj = GMLP_DIM // tn
    gpt = tn // GMLP_GROUP_DIM
    ucol = lambda j: jnp.maximum(j - nj, 0)
    return pl.pallas_call(
        functools.partial(_sgu_kernel, nj=nj),
        grid=(b, s // tm, 2 * nj),
        in_specs=[
            pl.BlockSpec((1, tm, d), lambda bi, i, j: (bi, i, 0)),
            pl.BlockSpec((1, d), lambda bi, i, j: (0, 0)),
            pl.BlockSpec((d, tn), lambda bi, i, j: (0, j)),
            pl.BlockSpec((1, tn), lambda bi, i, j: (0, ucol(j))),
            pl.BlockSpec((gpt, CHUNK, CHUNK), lambda bi, i, j: (ucol(j), 0, 0)),
            pl.BlockSpec((gpt, CHUNK, GMLP_GROUP_DIM), lambda bi, i, j: (ucol(j), 0, 0)),
        ],
        out_specs=pl.BlockSpec((1, tm, tn), lambda bi, i, j: (bi, i, ucol(j))),
        out_shape=jax.ShapeDtypeStruct((b, s, GMLP_DIM), BF16),
        scratch_shapes=[
            pltpu.VMEM((tm, d), BF16),
            pltpu.VMEM((nj, tm, tn), F32),
            pltpu.VMEM((tm, 1), F32),
        ],
        compiler_params=_params(("parallel", "parallel", "arbitrary")),
        name="sgu",
    )(x, mn, w_vu, gv, ws, bs_b)


def _mixout_kernel(x_ref, mn_ref, a_ref, c_ref, woa_ref, wog_ref, wga_ref, wgc_ref, ba_ref, bc_ref,
                   wout_ref, o_ref, h_ref, *, nj):
    j = pl.program_id(1)

    @pl.when(j == 0)
    def _():
        h_ref[...] = _rmsnorm(x_ref[...], mn_ref[...]).astype(BF16)
        o_ref[...] = jnp.zeros_like(o_ref)

    h = h_ref[...]
    ga = jax.nn.sigmoid(_dot(h, wga_ref[...]) + ba_ref[...])
    gc = jax.nn.sigmoid(_dot(h, wgc_ref[...]) + bc_ref[...])
    merged = ga * _dot(a_ref[...], woa_ref[...]) + gc * _dot(c_ref[...], wog_ref[...])
    o_ref[...] += _dot(merged.astype(BF16), wout_ref[...])

    @pl.when(j == nj - 1)
    def _():
        o_ref[...] = x_ref[...] + o_ref[...]


def _mixout(x, mn, a, c, woa, wog, wga, wgc, ba, bc, wout, *, tm=512, tn=256):
    m, d = x.shape
    tm = min(tm, m)
    nj = d // tn
    row = lambda i, j: (i, 0)
    col = lambda i, j: (0, j)
    return pl.pallas_call(
        functools.partial(_mixout_kernel, nj=nj),
        grid=(m // tm, nj),
        in_specs=[
            pl.BlockSpec((tm, d), row),
            pl.BlockSpec((1, d), lambda i, j: (0, 0)),
            pl.BlockSpec((tm, d), row),
            pl.BlockSpec((tm, d), row),
            pl.BlockSpec((d, tn), col),
            pl.BlockSpec((d, tn), col),
            pl.BlockSpec((d, tn), col),
            pl.BlockSpec((d, tn), col),
            pl.BlockSpec((1, tn), col),
            pl.BlockSpec((1, tn), col),
            pl.BlockSpec((tn, d), lambda i, j: (j, 0)),
        ],
        out_specs=pl.BlockSpec((tm, d), row),
        out_shape=jax.ShapeDtypeStruct((m, d), F32),
        scratch_shapes=[pltpu.VMEM((tm, d), BF16)],
        compiler_params=_params(("parallel", "arbitrary")),
        name="mixout",
    )(x, mn, a, c, woa, wog, wga, wgc, ba, bc, wout)


def _rot_cols(w):
    half = w.shape[-1] // 2
    return jnp.concatenate([-w[..., half:], w[..., :half]], axis=-1)


def _prepare(ffn1_norm, ffn1_w_gate, ffn1_w_up, ffn1_w_down, mix_norm, w_in, b_gate, q_norm, w_uq,
             kv_norm, w_uk, w_uv, w_o_attn, gmlp_norm, w_s, b_s, w_o_gmlp, w_out,
             ffn2_norm, ffn2_w_gate, ffn2_w_up, ffn2_w_down, final_norm):
    o1 = Q_LORA
    o2 = o1 + KV_LORA
    o3 = o2 + QK_ROPE
    o4 = o3 + 2 * GMLP_DIM
    win = w_in[0]
    row = lambda v: v.reshape(1, -1)
    w_kr = win[:, o2:o3]
    wqkr = jnp.concatenate([win[:, :o3], _rot_cols(w_kr)], axis=1).astype(BF16)
    wq = w_uq[0].reshape(Q_LORA, N_HEADS, QK_HEAD)
    wq_rope = wq[:, :, QK_NOPE:]
    wuq = jnp.concatenate([wq[:, :, :QK_NOPE], wq_rope, _rot_cols(wq_rope)], axis=-1)
    wuq = wuq.reshape(Q_LORA, N_HEADS * QK_PAD).astype(BF16)
    w_vu = jnp.concatenate([win[:, o3 + GMLP_DIM:o4], win[:, o3:o3 + GMLP_DIM]], axis=1).astype(BF16)
    bs_b = jnp.broadcast_to(b_s[0][:, :, None], (GMLP_GROUPS, CHUNK, GMLP_GROUP_DIM))
    return dict(
        ffn1=(row(ffn1_norm[0]), ffn1_w_gate[0].astype(BF16), ffn1_w_up[0].astype(BF16),
              ffn1_w_down[0].astype(BF16)),
        ffn2=(row(ffn2_norm[0]), ffn2_w_gate[0].astype(BF16), ffn2_w_up[0].astype(BF16),
              ffn2_w_down[0].astype(BF16)),
        final=row(final_norm),
        mn=row(mix_norm[0]),
        proj=(wqkr, row(q_norm[0]), wuq, row(kv_norm[0]), w_uk[0].astype(BF16),
              w_uv[0].T.astype(BF16)),
        sgu=(w_vu, row(gmlp_norm[0]), w_s[0].astype(BF16), bs_b),
        mix=(w_o_attn[0].astype(BF16), w_o_gmlp[0].astype(BF16),
             win[:, o4:o4 + D_MODEL].astype(BF16), win[:, o4 + D_MODEL:].astype(BF16),
             row(b_gate[0, :D_MODEL]), row(b_gate[0, D_MODEL:]), w_out[0].astype(BF16)),
    )


def _rope_table(s):
    inv = 1.0 / (ROPE_THETA ** (jnp.arange(0, QK_ROPE, 2, dtype=F32) / QK_ROPE))
    ang = jnp.arange(s, dtype=F32)[:, None] * inv[None, :]
    cos, sin = jnp.cos(ang), jnp.sin(ang)
    return jnp.concatenate([cos, cos, sin, sin], axis=1)


def _encoder(x, w):
    b, s, d = x.shape
    x1 = _ffn(x.reshape(b * s, d), *w["ffn1"], w["final"], final_norm=False)
    x1b = x1.reshape(b, s, d)
    q, k, vt = _proj(x1b, w["mn"], *w["proj"], _rope_table(s))
    attn = _attn(q, k, vt)
    sgu = _sgu(x1b, w["mn"], *w["sgu"])
    x2 = _mixout(x1, w["mn"], attn.reshape(b * s, d), sgu.reshape(b * s, d), *w["mix"])
    y = _ffn(x2, *w["ffn2"], w["final"], final_norm=True)
    return y.reshape(b, s, d)


def kernel(x_prompt, x_sample, ffn1_norm, ffn1_w_gate, ffn1_w_up, ffn1_w_down, mix_norm, w_in, b_gate, q_norm, w_uq, kv_norm, w_uk, w_uv, w_o_attn, gmlp_norm, w_s, b_s, w_o_gmlp, w_out, ffn2_norm, ffn2_w_gate, ffn2_w_up, ffn2_w_down, final_norm):
    w = _prepare(ffn1_norm, ffn1_w_gate, ffn1_w_up, ffn1_w_down, mix_norm, w_in, b_gate, q_norm,
                 w_uq, kv_norm, w_uk, w_uv, w_o_attn, gmlp_norm, w_s, b_s, w_o_gmlp, w_out,
                 ffn2_norm, ffn2_w_gate, ffn2_w_up, ffn2_w_down, final_norm)
    return (_encoder(x_prompt, w), _encoder(x_sample, w))
```
